```python
import jax, jax.numpy as jnp
from jax import lax
import numpy as np

D_MODEL = 1024
BATCH = 8
SEQ = 2048
DEPTH = 4
DEC_BATCH = 128
DEC_SEQ = 1
PAST_LEN = 16384
PAGE_SIZE = 128

D_MIX = D_MODEL
D_RET = D_MIX // 2
H_RET = 4
DK = D_RET // H_RET
DV = D_RET // H_RET
D_POOL = D_MIX - D_RET
POOL_WINDOWS = (2, 4, 8, 16)
N_POOL_GROUPS = len(POOL_WINDOWS)
POOL_GC = D_POOL // N_POOL_GROUPS
POOL_BUF = max(POOL_WINDOWS) - 1
D_IN = 4 * D_RET + D_POOL
D_FF = ((8 * D_MODEL // 3 + 255) // 256) * 256
CHUNK = 128
ROPE_THETA = 10000.0
EPS = 1e-6

kernel_name = "hybrid_retention_pool_decoder_step"


def rmsnorm(x, g):
    xf = x.astype(jnp.float32)
    r = xf * lax.rsqrt(jnp.mean(xf * xf, axis=-1, keepdims=True) + EPS)
    return (r * g.astype(jnp.float32)).astype(x.dtype)


def modulate(h, shift, scale):
    return h * (1.0 + scale[:, None, :]) + shift[:, None, :]


def rope(x, pos):
    d = x.shape[-1]
    inv = 1.0 / (ROPE_THETA ** (jnp.arange(0, d, 2, dtype=jnp.float32) / d))
    ang = pos.astype(jnp.float32)[:, None] * inv[None, :]
    cos = jnp.cos(ang)[None, :, None, :]
    sin = jnp.sin(ang)[None, :, None, :]
    xf = x.astype(jnp.float32)
    x1, x2 = xf[..., : d // 2], xf[..., d // 2:]
    return jnp.concatenate([x1 * cos - x2 * sin, x1 * sin + x2 * cos], axis=-1)


def retention(q, k, v, S0):
    B, T, H, _ = q.shape
    C = CHUNK if T % CHUNK == 0 else T
    NC = T // C
    log_gamma = jnp.log(1.0 - 2.0 ** (-5.0 - jnp.arange(H, dtype=jnp.float32)))
    idx = jnp.arange(C, dtype=jnp.float32)
    diff = idx[:, None] - idx[None, :]
    dmask = jnp.where(diff[None] >= 0,
                      jnp.exp(jnp.maximum(diff, 0.0)[None] * log_gamma[:, None, None]), 0.0)
    decay_q = jnp.exp((idx + 1.0)[None, :] * log_gamma[:, None])
    decay_k = jnp.exp((C - 1.0 - idx)[None, :] * log_gamma[:, None])
    decay_c = jnp.exp(C * log_gamma)

    def to_chunks(a):
        return jnp.moveaxis(a.astype(jnp.float32).reshape(B, NC, C, H, a.shape[-1]), 1, 0)

    def step(S, qkv):
        qc, kc, vc = qkv
        s = jnp.einsum('bihd,bjhd->bhij', qc, kc) * dmask[None]
        inner = jnp.einsum('bhij,bjhe->bihe', s, vc)
        cross = jnp.einsum('bihd,bhde->bihe', qc, S) * jnp.transpose(decay_q)[None, :, :, None]
        S_new = S * decay_c[None, :, None, None] + jnp.einsum('bjhd,bjhe,hj->bhde', kc, vc, decay_k)
        return S_new, inner + cross

    S_fin, o = lax.scan(step, S0.astype(jnp.float32), (to_chunks(q), to_chunks(k), to_chunks(v)))
    o = jnp.moveaxis(o, 0, 1).reshape(B, T, H, v.shape[-1])
    return o, S_fin


def pool_mixer(u, buf, n_past, pool_w, pool_scale):
    B, T, _ = u.shape
    xp = jnp.concatenate([buf.astype(u.dtype), u], axis=1).astype(jnp.float32)
    cs = jnp.concatenate([jnp.zeros((B, 1, D_POOL), jnp.float32), jnp.cumsum(xp, axis=1)], axis=1)
    end = cs[:, POOL_BUF + 1:]
    cur = xp[:, POOL_BUF:]
    t = jnp.arange(T)
    outs = []
    for g, w in enumerate(POOL_WINDOWS):
        lo, hi = g * POOL_GC, (g + 1) * POOL_GC
        start = cs[:, POOL_BUF + 1 - w: POOL_BUF + 1 - w + T, lo:hi]
        cnt = jnp.minimum(t + n_past + 1, w).astype(jnp.float32)
        outs.append((end[..., lo:hi] - start) / cnt[None, :, None] - cur[..., lo:hi])
    p = jnp.stack(outs, axis=2)
    y = jnp.einsum('btgc,gcd->btgd', p, pool_w.astype(jnp.float32)).reshape(B, T, D_POOL)
    y = y * pool_scale.astype(jnp.float32)
    new_buf = xp[:, -POOL_BUF:].astype(u.dtype)
    return y, new_buf


def layer(x, c, S0, buf, n_past, norm_mix, norm_ffn, w_ada, b_ada, w_in, ret_gn,
          pool_w, pool_scale, w_out, w_gu, w_down):
    B, T, _ = x.shape
    mod = jax.nn.silu(c) @ w_ada + b_ada
    shift_m, scale_m, gate_m, shift_f, scale_f, gate_f = jnp.split(mod, 6, axis=-1)

    h = modulate(rmsnorm(x, norm_mix), shift_m, scale_m)
    z = h @ w_in
    q, k, v, g, u = jnp.split(z, [D_RET, 2 * D_RET, 3 * D_RET, 4 * D_RET], axis=-1)
    pos = n_past + jnp.arange(T, dtype=jnp.int32)
    q = rope(q.reshape(B, T, H_RET, DK), pos)
    k = rope(k.reshape(B, T, H_RET, DK), pos) * (DK ** -0.5)
    v = v.reshape(B, T, H_RET, DV)
    o, S_new = retention(q, k, v, S0)
    o = rmsnorm(o, ret_gn.reshape(H_RET, DV)).reshape(B, T, D_RET)
    o = (o * jax.nn.silu(g.astype(jnp.float32))).astype(x.dtype)
    p, buf_new = pool_mixer(u, buf, n_past, pool_w, pool_scale)
    mix = jnp.concatenate([o, p.astype(x.dtype)], axis=-1) @ w_out
    x = x + gate_m[:, None, :] * mix

    h2 = modulate(rmsnorm(x, norm_ffn), shift_f, scale_f)
    a, b = jnp.split(h2 @ w_gu, 2, axis=-1)
    x = x + gate_f[:, None, :] * ((jax.nn.silu(a) * b) @ w_down)
    return x, S_new, buf_new


def setup_inputs(seed: int = 0) -> dict:
    key = jax.random.key(seed)
    ks = jax.random.split(key, 20)
    f32 = jnp.float32
    nrm = lambda k, s, sc: jax.random.normal(k, s, f32) * sc
    return {
        "x_prompt": nrm(ks[0], (BATCH, SEQ, D_MODEL), 1.0),
        "x_sample": nrm(ks[1], (DEC_BATCH, DEC_SEQ, D_MODEL), 1.0),
        "c_prompt": nrm(ks[2], (BATCH, D_MODEL), 1.0),
        "c_sample": nrm(ks[3], (DEC_BATCH, D_MODEL), 1.0),
        "state_ret": nrm(ks[4], (DEPTH, DEC_BATCH, H_RET, DK, DV), 1.0),
        "state_pool": nrm(ks[5], (DEPTH, DEC_BATCH, POOL_BUF, D_POOL), 1.0),
        "norm_mix": 1.0 + nrm(ks[6], (DEPTH, D_MODEL), 0.02),
        "norm_ffn": 1.0 + nrm(ks[7], (DEPTH, D_MODEL), 0.02),
        "w_ada": nrm(ks[8], (DEPTH, D_MODEL, 6 * D_MODEL), 0.5 * D_MODEL ** -0.5),
        "b_ada": nrm(ks[9], (DEPTH, 6 * D_MODEL), 0.01),
        "w_in": nrm(ks[10], (DEPTH, D_MODEL, D_IN), D_MODEL ** -0.5),
        "ret_gn": 1.0 + nrm(ks[11], (DEPTH, D_RET), 0.02),
        "pool_w": nrm(ks[12], (DEPTH, N_POOL_GROUPS, POOL_GC, POOL_GC), POOL_GC ** -0.5),
        "pool_scale": 1.0 + nrm(ks[13], (DEPTH, D_POOL), 0.02),
        "w_out": nrm(ks[14], (DEPTH, D_MIX, D_MODEL), D_MIX ** -0.5),
        "w_gu": nrm(ks[15], (DEPTH, D_MODEL, 2 * D_FF), D_MODEL ** -0.5),
        "w_down": nrm(ks[16], (DEPTH, D_FF, D_MODEL), D_FF ** -0.5),
        "final_norm": 1.0 + nrm(ks[17], (D_MODEL,), 0.02),
    }


def reference(x_prompt, x_sample, c_prompt, c_sample, state_ret, state_pool,
              norm_mix, norm_ffn, w_ada, b_ada, w_in, ret_gn, pool_w, pool_scale,
              w_out, w_gu, w_down, final_norm):
    xp, xs = x_prompt, x_sample
    S_p0 = jnp.zeros((BATCH, H_RET, DK, DV), jnp.float32)
    buf_p0 = jnp.zeros((BATCH, POOL_BUF, D_POOL), x_prompt.dtype)
    ret_p, pool_p, ret_s, pool_s = [], [], [], []
    for l in range(DEPTH):
        params = (norm_mix[l], norm_ffn[l], w_ada[l], b_ada[l], w_in[l], ret_gn[l],
                  pool_w[l], pool_scale[l], w_out[l], w_gu[l], w_down[l])
        xp, Sp, bp = layer(xp, c_prompt, S_p0, buf_p0, 0, *params)
        xs, Ss, bs = layer(xs, c_sample, state_ret[l], state_pool[l], PAST_LEN, *params)
        ret_p.append(Sp)
        pool_p.append(bp)
        ret_s.append(Ss)
        pool_s.append(bs)
    y_prompt = rmsnorm(xp, final_norm)
    y_sample = rmsnorm(xs, final_norm)
    new_ret_prompt = jnp.stack(ret_p, axis=0)
    new_pool_prompt = jnp.stack(pool_p, axis=0)
    new_ret_sample = jnp.stack(ret_s, axis=0)
    new_pool_sample = jnp.stack(pool_s, axis=0)
    return (y_prompt, y_sample, new_ret_prompt, new_pool_prompt, new_ret_sample, new_pool_sample)
```

```python
import functools

import jax
import jax.numpy as jnp
from jax import lax
from jax.experimental import pallas as pl
from jax.experimental.pallas import tpu as pltpu

F32 = jnp.float32
BF16 = jnp.bfloat16

D_MODEL = 1024
DEPTH = 4
H_RET = 4
HEAD = 128
D_RET = H_RET * HEAD
D_POOL = D_MODEL - D_RET
POOL_WINDOWS = (2, 4, 8, 16)
POOL_GC = D_POOL // len(POOL_WINDOWS)
POOL_BUF = max(POOL_WINDOWS) - 1
HALO = 16
D_IN = 4 * D_RET + D_POOL
D_FF = 2816
CHUNK = 128
ROPE_THETA = 10000.0
EPS = 1e-6
N_MOD = 6

FF_CHUNKS = ((0, 512), (512, 512), (1024, 512), (1536, 512), (2048, 512), (2560, 256))

VMEM_LIMIT_BYTES = 56 * 1024 * 1024


def _silu(x):
    return x * jax.nn.sigmoid(x)


def _rmsnorm_mod(x, gain, shift, scale):
    ms = jnp.mean(x * x, axis=-1, keepdims=True)
    h = x * lax.rsqrt(ms + EPS) * gain
    return h * (1.0 + scale) + shift


def _const_spec(shape, index_map):
    return pl.BlockSpec(shape, index_map, pipeline_mode=pl.Buffered(1))


def _mod_kernel(c_ref, w_ref, b_ref, o_ref):
    sc = _silu(c_ref[...]).astype(BF16)
    o_ref[...] = jnp.dot(sc, w_ref[...].astype(BF16), preferred_element_type=F32) + b_ref[...]


def _modulation(c_all, w_ada, b_ada):
    n = c_all.shape[0]
    tn = 1024
    return pl.pallas_call(
        _mod_kernel,
        out_shape=jax.ShapeDtypeStruct((DEPTH, n, N_MOD * D_MODEL), F32),
        grid=(DEPTH, N_MOD * D_MODEL // tn),
        in_specs=[
            pl.BlockSpec((n, D_MODEL), lambda l, j: (0, 0)),
            pl.BlockSpec((None, D_MODEL, tn), lambda l, j: (l, 0, j)),
            pl.BlockSpec((None, 1, tn), lambda l, j: (l, 0, j)),
        ],
        out_specs=pl.BlockSpec((None, n, tn), lambda l, j: (l, 0, j)),
        compiler_params=pltpu.CompilerParams(
            dimension_semantics=("arbitrary", "arbitrary"), vmem_limit_bytes=VMEM_LIMIT_BYTES),
        name="adaln_modulation",
    )(c_all, w_ada, b_ada.reshape(DEPTH, 1, N_MOD * D_MODEL))


def _rope_store(z, cos, sin, out_ref, post_scale):
    for hh in range(H_RET):
        cols = slice(hh * HEAD, (hh + 1) * HEAD)
        zh = z[:, cols]
        r = zh * cos + pltpu.roll(zh, HEAD // 2, 1) * sin
        if post_scale is not None:
            r = r * post_scale
        out_ref[:, cols] = r.astype(out_ref.dtype)


def _in_proj_kernel(x_ref, mod_ref, gain_ref, w_ref, cos_ref, sin_ref,
                    q_ref, k_ref, v_ref, g_ref, u_ref, *, per_row_mod):
    if per_row_mod:
        shift, scale = mod_ref[:, 0:D_MODEL], mod_ref[:, D_MODEL:2 * D_MODEL]
    else:
        shift, scale = mod_ref[0], mod_ref[1]
    h = _rmsnorm_mod(x_ref[...], gain_ref[...], shift, scale).astype(BF16)
    cos, sin = cos_ref[...], sin_ref[...]

    def proj(sec):
        return jnp.dot(h, w_ref[:, sec * D_RET:(sec + 1) * D_RET], preferred_element_type=F32)

    _rope_store(proj(0), cos, sin, q_ref, None)
    _rope_store(proj(1), cos, sin, k_ref, HEAD ** -0.5)
    v_ref[...] = proj(2).astype(v_ref.dtype)
    g_ref[...] = proj(3)
    u_ref[...] = proj(4)


def _in_proj_prompt(x, mod_p, gain, w_in, cos, sin, layer, tm):
    nb, t, _ = x.shape
    row = lambda b, i: (b, i, 0)
    tab = lambda b, i: (i, 0)
    outs = [jax.ShapeDtypeStruct((nb, t, D_RET), BF16)] * 3 + [jax.ShapeDtypeStruct((nb, t, D_RET), F32)] * 2
    return pl.pallas_call(
        functools.partial(_in_proj_kernel, per_row_mod=False),
        out_shape=outs,
        grid=(nb, t // tm),
        in_specs=[
            pl.BlockSpec((None, tm, D_MODEL), row),
            pl.BlockSpec((None, None, N_MOD, 1, D_MODEL), lambda b, i: (layer, b, 0, 0, 0)),
            _const_spec((None, 1, D_MODEL), lambda b, i: (layer, 0, 0)),
            _const_spec((None, D_MODEL, D_IN), lambda b, i: (layer, 0, 0)),
            pl.BlockSpec((tm, HEAD), tab),
            pl.BlockSpec((tm, HEAD), tab),
        ],
        out_specs=[pl.BlockSpec((None, tm, D_RET), row)] * 5,
        compiler_params=pltpu.CompilerParams(
            dimension_semantics=("arbitrary", "arbitrary"), vmem_limit_bytes=VMEM_LIMIT_BYTES),
        name="in_proj_prompt",
    )(x, mod_p, gain, w_in, cos, sin)


def _in_proj_sample(x, mod_s, gain, w_in, cos, sin, layer):
    n = x.shape[0]
    outs = [jax.ShapeDtypeStruct((n, D_RET), F32)] * 5
    return pl.pallas_call(
        functools.partial(_in_proj_kernel, per_row_mod=True),
        out_shape=outs,
        grid=(1,),
        in_specs=[
            pl.BlockSpec((n, D_MODEL), lambda i: (0, 0)),
            pl.BlockSpec((None, n, N_MOD * D_MODEL), lambda i: (layer, 0, 0)),
            pl.BlockSpec((None, 1, D_MODEL), lambda i: (layer, 0, 0)),
            _const_spec((None, D_MODEL, D_IN), lambda i: (layer, 0, 0)),
            pl.BlockSpec((1, HEAD), lambda i: (0, 0)),
            pl.BlockSpec((1, HEAD), lambda i: (0, 0)),
        ],
        out_specs=[pl.BlockSpec((n, D_RET), lambda i: (0, 0))] * 5,
        compiler_params=pltpu.CompilerParams(
            dimension_semantics=("arbitrary",), vmem_limit_bytes=VMEM_LIMIT_BYTES),
        name="in_proj_sample",
    )(x, mod_s, gain, w_in, cos, sin)


def _head_norm_gate(o, g, gain):
    ms = jnp.mean(o * o, axis=-1, keepdims=True)
    return o * lax.rsqrt(ms + EPS) * gain * _silu(g)


def _ret_prompt_kernel(q_ref, k_ref, v_ref, g_ref, dmask_ref, dq_ref, dk_ref, dc_ref, gn_ref,
                       o_ref, sfin_ref, s_ref, *, n_chunks):
    i = pl.program_id(1)

    @pl.when(i == 0)
    def _():
        s_ref[...] = jnp.zeros_like(s_ref)

    for c in range(n_chunks):
        rows = slice(c * CHUNK, (c + 1) * CHUNK)
        for hh in range(H_RET):
            cols = slice(hh * HEAD, (hh + 1) * HEAD)
            q, k, v = q_ref[rows, cols], k_ref[rows, cols], v_ref[rows, cols]
            state = s_ref[hh]
            s = lax.dot_general(q, k, (((1,), (1,)), ((), ())), preferred_element_type=F32) * dmask_ref[hh]
            inner = jnp.dot(s.astype(BF16), v, preferred_element_type=F32)
            cross = jnp.dot(q, state.astype(BF16), preferred_element_type=F32) * dq_ref[hh]
            kd = (k.astype(F32) * dk_ref[hh]).astype(BF16)
            s_ref[hh] = state * dc_ref[hh] + lax.dot_general(
                kd, v, (((0,), (0,)), ((), ())), preferred_element_type=F32)
            o_ref[rows, cols] = _head_norm_gate(inner + cross, g_ref[rows, cols], gn_ref[:, cols]).astype(o_ref.dtype)

    @pl.when(i == pl.num_programs(1) - 1)
    def _():
        sfin_ref[...] = s_ref[...]


def _ret_prompt(q, k, v, g, tabs, ret_gn, layer, tb):
    nb, t, _ = q.shape
    dmask, dq, dk, dc = tabs
    row = lambda b, i: (b, i, 0)
    c3 = lambda b, i: (0, 0, 0)
    return pl.pallas_call(
        functools.partial(_ret_prompt_kernel, n_chunks=tb // CHUNK),
        out_shape=[jax.ShapeDtypeStruct((nb, t, D_RET), BF16),
                   jax.ShapeDtypeStruct((nb, H_RET, HEAD, HEAD), F32)],
        grid=(nb, t // tb),
        in_specs=[pl.BlockSpec((None, tb, D_RET), row)] * 4 + [
            _const_spec((H_RET, CHUNK, CHUNK), c3),
            _const_spec((H_RET, CHUNK, 1), c3),
            _const_spec((H_RET, CHUNK, 1), c3),
            _const_spec((H_RET, 1, HEAD), c3),
            _const_spec((None, 1, D_RET), lambda b, i: (layer, 0, 0)),
        ],
        out_specs=[pl.BlockSpec((None, tb, D_RET), row),
                   pl.BlockSpec((None, H_RET, HEAD, HEAD), lambda b, i: (b, 0, 0, 0))],
        scratch_shapes=[pltpu.VMEM((H_RET, HEAD, HEAD), F32)],
        compiler_params=pltpu.CompilerParams(
            dimension_semantics=("arbitrary", "arbitrary"), vmem_limit_bytes=VMEM_LIMIT_BYTES),
        name="retention_prompt",
    )(q, k, v, g, dmask, dq, dk, dc, ret_gn)


def _ret_sample_kernel(qt_ref, kt_ref, q_ref, k_ref, v_ref, g_ref, gamma_ref, gn_ref, s_ref,
                       o_ref, snew_ref, *, bb):
    for j in range(bb):
        for hh in range(H_RET):
            cols = slice(hh * HEAD, (hh + 1) * HEAD)
            gamma = gamma_ref[hh]
            state = s_ref[j, hh]
            q_col = qt_ref[cols, j:j + 1]
            k_col = kt_ref[cols, j:j + 1]
            q_row, k_row, v_row = q_ref[j:j + 1, cols], k_ref[j:j + 1, cols], v_ref[j:j + 1, cols]
            cross = jnp.sum(q_col * state, axis=0, keepdims=True) * gamma
            inner = jnp.sum(q_row * k_row, axis=-1, keepdims=True) * v_row
            snew_ref[j, hh] = state * gamma + k_col * v_row
            o_ref[j:j + 1, cols] = _head_norm_gate(inner + cross, g_ref[j:j + 1, cols], gn_ref[:, cols])


def _ret_sample(q, k, v, g, gamma, ret_gn, state, layer, bb):
    n = q.shape[0]
    nblk = n // bb
    qt = q.reshape(nblk, bb, D_RET).transpose(0, 2, 1)
    kt = k.reshape(nblk, bb, D_RET).transpose(0, 2, 1)
    rows = pl.BlockSpec((bb, D_RET), lambda i: (i, 0))
    return pl.pallas_call(
        functools.partial(_ret_sample_kernel, bb=bb),
        out_shape=[jax.ShapeDtypeStruct((n, D_RET), F32),
                   jax.ShapeDtypeStruct((n, H_RET, HEAD, HEAD), F32)],
        grid=(nblk,),
        in_specs=[
            pl.BlockSpec((None, D_RET, bb), lambda i: (i, 0, 0)),
            pl.BlockSpec((None, D_RET, bb), lambda i: (i, 0, 0)),
            rows, rows, rows, rows,
            pl.BlockSpec((H_RET, 1, HEAD), lambda i: (0, 0, 0)),
            pl.BlockSpec((None, 1, D_RET), lambda i: (layer, 0, 0)),
            pl.BlockSpec((None, bb, H_RET, HEAD, HEAD), lambda i: (layer, i, 0, 0, 0)),
        ],
        out_specs=[rows, pl.BlockSpec((bb, H_RET, HEAD, HEAD), lambda i: (i, 0, 0, 0))],
        compiler_params=pltpu.CompilerParams(
            dimension_semantics=("arbitrary",), vmem_limit_bytes=VMEM_LIMIT_BYTES),
        name="retention_sample",
    )(qt, kt, q, k, v, g, gamma, ret_gn, state)


def _pool_project(window_sums, cur, inv_cnt, pw_ref, pscale_ref):
    ys = []
    for gi in range(len(POOL_WINDOWS)):
        cols = slice(gi * POOL_GC, (gi + 1) * POOL_GC)
        p = window_sums[gi] * inv_cnt[gi] - cur[:, cols]
        y = jnp.dot(p.astype(BF16), pw_ref[gi], preferred_element_type=F32) * pscale_ref[:, cols]
        ys.append(y.astype(BF16))
    return jnp.concatenate(ys, axis=-1)


def _mix_ffn(x, o, y, gate_m, gain_f, shift_f, scale_f, gate_f, wo_ref, wgu_ref, wd_ref, hm_ref):
    mix = jnp.dot(jnp.concatenate([o, y], axis=-1), wo_ref[...], preferred_element_type=F32)
    x1 = x + gate_m * mix
    h2 = _rmsnorm_mod(x1, gain_f, shift_f, scale_f).astype(BF16)
    for off, width in FF_CHUNKS:
        a = jnp.dot(h2, wgu_ref[:, off:off + width], preferred_element_type=F32)
        b = jnp.dot(h2, wgu_ref[:, D_FF + off:D_FF + off + width], preferred_element_type=F32)
        hm_ref[:, off:off + width] = (_silu(a) * b).astype(BF16)
    return x1 + gate_f * jnp.dot(hm_ref[...], wd_ref[...], preferred_element_type=F32)


def _mix_prompt_kernel(x_ref, o_ref, u_ref, mod_ref, gain_ref, pw_ref, pscale_ref, wo_ref, wgu_ref, wd_ref,
                       out_ref, pool_ref, ext_ref, carry_ref, hm_ref, *, tm):
    i = pl.program_id(1)

    @pl.when(i == 0)
    def _():
        carry_ref[...] = jnp.zeros_like(carry_ref)

    u = u_ref[...]
    ext_ref[0:HALO, :] = carry_ref[...]
    ext_ref[HALO:, :] = u
    carry_ref[...] = u[tm - HALO:, :]

    @pl.when(i == pl.num_programs(1) - 1)
    def _():
        pool_ref[...] = u[tm - HALO:, :]

    pos = i * tm + lax.broadcasted_iota(jnp.int32, (tm, 1), 0)
    sums, inv_cnt = [], []
    for gi, w in enumerate(POOL_WINDOWS):
        acc = ext_ref[:, gi * POOL_GC:(gi + 1) * POOL_GC]
        span = 1
        while span < w:
            acc = acc + pltpu.roll(acc, span, 0)
            span *= 2
        sums.append(acc[HALO:, :])
        inv_cnt.append(1.0 / jnp.minimum(pos + 1, w).astype(F32))
    y = _pool_project(sums, u, inv_cnt, pw_ref, pscale_ref)
    out_ref[...] = _mix_ffn(x_ref[...], o_ref[...], y, mod_ref[2], gain_ref[...], mod_ref[3], mod_ref[4],
                            mod_ref[5], wo_ref, wgu_ref, wd_ref, hm_ref)


def _mix_prompt(x, o, u, mod_p, gain_f, pool_w, pool_scale, w_out, w_gu, w_down, layer, tm):
    nb, t, _ = x.shape
    row = lambda b, i: (b, i, 0)
    lay3 = lambda b, i: (layer, 0, 0)
    return pl.pallas_call(
        functools.partial(_mix_prompt_kernel, tm=tm),
        out_shape=[jax.ShapeDtypeStruct((nb, t, D_MODEL), F32),
                   jax.ShapeDtypeStruct((nb, HALO, D_POOL), F32)],
        grid=(nb, t // tm),
        in_specs=[
            pl.BlockSpec((None, tm, D_MODEL), row),
            pl.BlockSpec((None, tm, D_RET), row),
            pl.BlockSpec((None, tm, D_POOL), row),
            pl.BlockSpec((None, None, N_MOD, 1, D_MODEL), lambda b, i: (layer, b, 0, 0, 0)),
            _const_spec((None, 1, D_MODEL), lay3),
            _const_spec((None, len(POOL_WINDOWS), POOL_GC, POOL_GC), lambda b, i: (layer, 0, 0, 0)),
            _const_spec((None, 1, D_POOL), lay3),
            _const_spec((None, D_MODEL, D_MODEL), lay3),
            _const_spec((None, D_MODEL, 2 * D_FF), lay3),
            _const_spec((None, D_FF, D_MODEL), lay3),
        ],
        out_specs=[pl.BlockSpec((None, tm, D_MODEL), row),
                   pl.BlockSpec((None, HALO, D_POOL), lambda b, i: (b, 0, 0))],
        scratch_shapes=[pltpu.VMEM((tm + HALO, D_POOL), F32),
                        pltpu.VMEM((HALO, D_POOL), F32),
                        pltpu.VMEM((tm, D_FF), BF16)],
        compiler_params=pltpu.CompilerParams(
            dimension_semantics=("arbitrary", "arbitrary"), vmem_limit_bytes=VMEM_LIMIT_BYTES),
        name="mix_ffn_prompt",
    )(x, o, u, mod_p, gain_f, pool_w, pool_scale, w_out, w_gu, w_down)


def _mix_sample_kernel(x_ref, o_ref, u_ref, buf_ref, mod_ref, gain_ref, pw_ref, pscale_ref, wo_ref, wgu_ref,
                       wd_ref, out_ref, newbuf_ref, hm_ref):
    u = u_ref[...]
    sums, inv_cnt = [], []
    for gi, w in enumerate(POOL_WINDOWS):
        acc = u[:, gi * POOL_GC:(gi + 1) * POOL_GC]
        for r in range(POOL_BUF - (w - 1), POOL_BUF):
            acc = acc + buf_ref[:, r * D_POOL + gi * POOL_GC:r * D_POOL + (gi + 1) * POOL_GC]
        sums.append(acc)
        inv_cnt.append(1.0 / w)
    y = _pool_project(sums, u, inv_cnt, pw_ref, pscale_ref)
    newbuf_ref[:, 0:(POOL_BUF - 1) * D_POOL] = buf_ref[:, D_POOL:]
    newbuf_ref[:, (POOL_BUF - 1) * D_POOL:] = u
    m = lambda k: mod_ref[:, k * D_MODEL:(k + 1) * D_MODEL]
    out_ref[...] = _mix_ffn(x_ref[...], o_ref[...].astype(BF16), y, m(2), gain_ref[...], m(3), m(4), m(5),
                            wo_ref, wgu_ref, wd_ref, hm_ref)


def _mix_sample(x, o, u, buf, mod_s, gain_f, pool_w, pool_scale, w_out, w_gu, w_down, layer):
    n = x.shape[0]
    full = lambda i: (0, 0)
    lay3 = lambda i: (layer, 0, 0)
    return pl.pallas_call(
        _mix_sample_kernel,
        out_shape=[jax.ShapeDtypeStruct((n, D_MODEL), F32),
                   jax.ShapeDtypeStruct((n, POOL_BUF * D_POOL), F32)],
        grid=(1,),
        in_specs=[
            pl.BlockSpec((n, D_MODEL), full),
            pl.BlockSpec((n, D_RET), full),
            pl.BlockSpec((n, D_POOL), full),
            _const_spec((None, n, POOL_BUF * D_POOL), lay3),
            _const_spec((None, n, N_MOD * D_MODEL), lay3),
            pl.BlockSpec((None, 1, D_MODEL), lay3),
            pl.BlockSpec((None, len(POOL_WINDOWS), POOL_GC, POOL_GC), lambda i: (layer, 0, 0, 0)),
            pl.BlockSpec((None, 1, D_POOL), lay3),
            _const_spec((None, D_MODEL, D_MODEL), lay3),
            _const_spec((None, D_MODEL, 2 * D_FF), lay3),
            _const_spec((None, D_FF, D_MODEL), lay3),
        ],
        out_specs=[pl.BlockSpec((n, D_MODEL), full),
                   pl.BlockSpec((n, POOL_BUF * D_POOL), full)],
        scratch_shapes=[pltpu.VMEM((n, D_FF), BF16)],
        compiler_params=pltpu.CompilerParams(
            dimension_semantics=("arbitrary",), vmem_limit_bytes=VMEM_LIMIT_BYTES),
        name="mix_ffn_sample",
    )(x, o, u, buf, mod_s, gain_f, pool_w, pool_scale, w_out, w_gu, w_down)


def _final_norm_kernel(x_ref, gain_ref, o_ref):
    x = x_ref[...]
    ms = jnp.mean(x * x, axis=-1, keepdims=True)
    o_ref[...] = x * lax.rsqrt(ms + EPS) * gain_ref[...]


def _final_norm(x2d, gain, tm):
    n = x2d.shape[0]
    return pl.pallas_call(
        _final_norm_kernel,
        out_shape=jax.ShapeDtypeStruct(x2d.shape, F32),
        grid=(n // tm,),
        in_specs=[pl.BlockSpec((tm, D_MODEL), lambda i: (i, 0)),
                  pl.BlockSpec((1, D_MODEL), lambda i: (0, 0))],
        out_specs=pl.BlockSpec((tm, D_MODEL), lambda i: (i, 0)),
        compiler_params=pltpu.CompilerParams(
            dimension_semantics=("arbitrary",), vmem_limit_bytes=VMEM_LIMIT_BYTES),
        name="final_norm",
    )(x2d, gain)


def _rope_tables(pos):
    inv = 1.0 / (ROPE_THETA ** (jnp.arange(0, HEAD, 2, dtype=F32) / HEAD))
    ang = pos.astype(F32)[:, None] * inv[None, :]
    cos, sin = jnp.cos(ang), jnp.sin(ang)
    return jnp.concatenate([cos, cos], axis=-1), jnp.concatenate([-sin, sin], axis=-1)


def _decay_tables():
    log_gamma = jnp.log(1.0 - 2.0 ** (-5.0 - jnp.arange(H_RET, dtype=F32)))
    idx = jnp.arange(CHUNK, dtype=F32)
    diff = idx[:, None] - idx[None, :]
    dmask = jnp.where(diff[None] >= 0, jnp.exp(jnp.maximum(diff, 0.0)[None] * log_gamma[:, None, None]), 0.0)
    dq = jnp.exp((idx + 1.0)[None, :] * log_gamma[:, None])[:, :, None]
    dk = jnp.exp((CHUNK - 1.0 - idx)[None, :] * log_gamma[:, None])[:, :, None]
    dc = jnp.broadcast_to(jnp.exp(CHUNK * log_gamma)[:, None, None], (H_RET, 1, HEAD))
    gamma1 = jnp.broadcast_to(jnp.exp(log_gamma)[:, None, None], (H_RET, 1, HEAD))
    return (dmask, dq, dk, dc), gamma1


def kernel(x_prompt, x_sample, c_prompt, c_sample, state_ret, state_pool, norm_mix, norm_ffn, w_ada, b_ada,
           w_in, ret_gn, pool_w, pool_scale, w_out, w_gu, w_down, final_norm):
    nb, t, _ = x_prompt.shape
    ns = x_sample.shape[0]
    past_len = 16384
    tm = 512

    mod = _modulation(jnp.concatenate([c_prompt, c_sample], axis=0), w_ada, b_ada)
    mod_p = mod[:, :nb].reshape(DEPTH, nb, N_MOD, 1, D_MODEL)
    mod_s = mod[:, nb:]

    w_in_b, w_out_b, w_gu_b, w_down_b, pool_w_b = (w.astype(BF16) for w in (w_in, w_out, w_gu, w_down, pool_w))
    gain_m = norm_mix.reshape(DEPTH, 1, D_MODEL)
    gain_f = norm_ffn.reshape(DEPTH, 1, D_MODEL)
    gn = ret_gn.reshape(DEPTH, 1, D_RET)
    pscale = pool_scale.reshape(DEPTH, 1, D_POOL)
    cos_p, sin_p = _rope_tables(jnp.arange(t, dtype=jnp.int32))
    cos_s, sin_s = _rope_tables(jnp.full((1,), past_len, dtype=jnp.int32))
    tabs, gamma1 = _decay_tables()
    pool_state = state_pool.reshape(DEPTH, ns, POOL_BUF * D_POOL)

    xp, xs = x_prompt, x_sample.reshape(ns, D_MODEL)
    ret_p, pool_p, ret_s, pool_s = [], [], [], []
    for l in range(DEPTH):
        q, k, v, g, u = _in_proj_prompt(xp, mod_p, gain_m, w_in_b, cos_p, sin_p, l, tm)
        o, s_fin = _ret_prompt(q, k, v, g, tabs, gn, l, tm)
        xp, pool16 = _mix_prompt(xp, o, u, mod_p, gain_f, pool_w_b, pscale, w_out_b, w_gu_b, w_down_b, l, tm)
        ret_p.append(s_fin)
        pool_p.append(pool16[:, HALO - POOL_BUF:, :])

        q, k, v, g, u = _in_proj_sample(xs, mod_s, gain_m, w_in_b, cos_s, sin_s, l)
        o, s_new = _ret_sample(q, k, v, g, gamma1, gn, state_ret, l, 8)
        xs, buf_new = _mix_sample(xs, o, u, pool_state, mod_s, gain_f, pool_w_b, pscale, w_out_b, w_gu_b,
                                  w_down_b, l)
        ret_s.append(s_new)
        pool_s.append(buf_new.reshape(ns, POOL_BUF, D_POOL))

    y_prompt = _final_norm(xp.reshape(nb * t, D_MODEL), final_norm.reshape(1, D_MODEL), 1024).reshape(nb, t, D_MODEL)
    y_sample = _final_norm(xs, final_norm.reshape(1, D_MODEL), ns).reshape(ns, 1, D_MODEL)
    return (y_prompt, y_sample, jnp.stack(ret_p), jnp.stack(pool_p), jnp.stack(ret_s), jnp.stack(pool_s))
```

```python
import functools

import jax
import jax.numpy as jnp
from jax import lax
from jax.experimental import pallas as pl
from jax.experimental.pallas import tpu as pltpu

F32 = jnp.float32
BF16 = jnp.bfloat16

D_MODEL = 1024
DEPTH = 4
H_RET = 4
HEAD = 128
D_RET = H_RET * HEAD
D_POOL = D_MODEL - D_RET
POOL_WINDOWS = (2, 4, 8, 16)
POOL_GC = D_POOL // len(POOL_WINDOWS)
POOL_BUF = max(POOL_WINDOWS) - 1
HALO = 16
D_IN = 4 * D_RET + D_POOL
D_FF = 2816
CHUNK = 128
ROPE_THETA = 10000.0
EPS = 1e-6
N_MOD = 6

FF_CHUNKS = ((0, 512), (512, 512), (1024, 512), (1536, 512), (2048, 512), (2560, 256))

VMEM_LIMIT_BYTES = 56 * 1024 * 1024


def _silu(x):
    return x * jax.nn.sigmoid(x)


def _rmsnorm_mod(x, gain, shift, scale):
    ms = jnp.mean(x * x, axis=-1, keepdims=True)
    h = x * lax.rsqrt(ms + EPS) * gain
    return h * (1.0 + scale) + shift


def _const_spec(shape, index_map):
    return pl.BlockSpec(shape, index_map, pipeline_mode=pl.Buffered(1))


def _mod_kernel(c_ref, w_ref, b_ref, o_ref):
    sc = _silu(c_ref[...]).astype(BF16)
    o_ref[...] = jnp.dot(sc, w_ref[...].astype(BF16), preferred_element_type=F32) + b_ref[...]


def _modulation(c_all, w_ada, b_ada):
    n = c_all.shape[0]
    tn = 1024
    return pl.pallas_call(
        _mod_kernel,
        out_shape=jax.ShapeDtypeStruct((DEPTH, n, N_MOD * D_MODEL), F32),
        grid=(DEPTH, N_MOD * D_MODEL // tn),
        in_specs=[
            pl.BlockSpec((n, D_MODEL), lambda l, j: (0, 0)),
            pl.BlockSpec((None, D_MODEL, tn), lambda l, j: (l, 0, j)),
            pl.BlockSpec((None, 1, tn), lambda l, j: (l, 0, j)),
        ],
        out_specs=pl.BlockSpec((None, n, tn), lambda l, j: (l, 0, j)),
        compiler_params=pltpu.CompilerParams(
            dimension_semantics=("arbitrary", "arbitrary"), vmem_limit_bytes=VMEM_LIMIT_BYTES),
        name="adaln_modulation",
    )(c_all, w_ada, b_ada.reshape(DEPTH, 1, N_MOD * D_MODEL))


def _rope_store(z, cos, sin, out_ref, post_scale):
    for hh in range(H_RET):
        cols = slice(hh * HEAD, (hh + 1) * HEAD)
        zh = z[:, cols]
        r = zh * cos + pltpu.roll(zh, HEAD // 2, 1) * sin
        if post_scale is not None:
            r = r * post_scale
        out_ref[:, cols] = r.astype(out_ref.dtype)


def _in_proj_kernel(x_ref, mod_ref, gain_ref, w_ref, cos_ref, sin_ref,
                    q_ref, k_ref, v_ref, g_ref, u_ref, *, per_row_mod):
    if per_row_mod:
        shift, scale = mod_ref[:, 0:D_MODEL], mod_ref[:, D_MODEL:2 * D_MODEL]
    else:
        shift, scale = mod_ref[0], mod_ref[1]
    h = _rmsnorm_mod(x_ref[...], gain_ref[...], shift, scale).astype(BF16)
    cos, sin = cos_ref[...], sin_ref[...]

    def proj(sec):
        return jnp.dot(h, w_ref[:, sec * D_RET:(sec + 1) * D_RET], preferred_element_type=F32)

    _rope_store(proj(0), cos, sin, q_ref, None)
    _rope_store(proj(1), cos, sin, k_ref, HEAD ** -0.5)
    v_ref[...] = proj(2).astype(v_ref.dtype)
    g_ref[...] = proj(3)
    u_ref[...] = proj(4)


def _in_proj_prompt(x, mod_p, gain, w_in, cos, sin, layer, tm):
    nb, t, _ = x.shape
    row = lambda b, i: (b, i, 0)
    tab = lambda b, i: (i, 0)
    outs = [jax.ShapeDtypeStruct((nb, t, D_RET), BF16)] * 3 + [jax.ShapeDtypeStruct((nb, t, D_RET), F32)] * 2
    return pl.pallas_call(
        functools.partial(_in_proj_kernel, per_row_mod=False),
        out_shape=outs,
        grid=(nb, t // tm),
        in_specs=[
            pl.BlockSpec((None, tm, D_MODEL), row),
            pl.BlockSpec((None, None, N_MOD, 1, D_MODEL), lambda b, i: (layer, b, 0, 0, 0)),
            _const_spec((None, 1, D_MODEL), lambda b, i: (layer, 0, 0)),
            _const_spec((None, D_MODEL, D_IN), lambda b, i: (layer, 0, 0)),
            pl.BlockSpec((tm, HEAD), tab),
            pl.BlockSpec((tm, HEAD), tab),
        ],
        out_specs=[pl.BlockSpec((None, tm, D_RET), row)] * 5,
        compiler_params=pltpu.CompilerParams(
            dimension_semantics=("arbitrary", "arbitrary"), vmem_limit_bytes=VMEM_LIMIT_BYTES),
        name="in_proj_prompt",
    )(x, mod_p, gain, w_in, cos, sin)


def _in_proj_sample(x, mod_s, gain, w_in, cos, sin, layer):
    n = x.shape[0]
    outs = [jax.ShapeDtypeStruct((n, D_RET), F32)] * 5
    return pl.pallas_call(
        functools.partial(_in_proj_kernel, per_row_mod=True),
        out_shape=outs,
        grid=(1,),
        in_specs=[
            pl.BlockSpec((n, D_MODEL), lambda i: (0, 0)),
            pl.BlockSpec((None, n, N_MOD * D_MODEL), lambda i: (layer, 0, 0)),
            pl.BlockSpec((None, 1, D_MODEL), lambda i: (layer, 0, 0)),
            _const_spec((None, D_MODEL, D_IN), lambda i: (layer, 0, 0)),
            pl.BlockSpec((1, HEAD), lambda i: (0, 0)),
            pl.BlockSpec((1, HEAD), lambda i: (0, 0)),
        ],
        out_specs=[pl.BlockSpec((n, D_RET), lambda i: (0, 0))] * 5,
        compiler_params=pltpu.CompilerParams(
            dimension_semantics=("arbitrary",), vmem_limit_bytes=VMEM_LIMIT_BYTES),
        name="in_proj_sample",
    )(x, mod_s, gain, w_in, cos, sin)


def _head_norm_gate(o, g, gain):
    ms = jnp.mean(o * o, axis=-1, keepdims=True)
    return o * lax.rsqrt(ms + EPS) * gain * _silu(g)


def _ret_prompt_kernel(q_ref, k_ref, v_ref, g_ref, dmask_ref, dq_ref, dk_ref, dc_ref, gn_ref,
                       o_ref, sfin_ref, s_ref, *, n_chunks):
    i = pl.program_id(1)

    @pl.when(i == 0)
    def _():
        s_ref[...] = jnp.zeros_like(s_ref)

    for c in range(n_chunks):
        rows = slice(c * CHUNK, (c + 1) * CHUNK)
        for hh in range(H_RET):
            cols = slice(hh * HEAD, (hh + 1) * HEAD)
            q, k, v = q_ref[rows, cols], k_ref[rows, cols], v_ref[rows, cols]
            state = s_ref[hh]
            s = lax.dot_general(q, k, (((1,), (1,)), ((), ())), preferred_element_type=F32) * dmask_ref[hh]
            inner = jnp.dot(s.astype(BF16), v, preferred_element_type=F32)
            cross = jnp.dot(q, state.astype(BF16), preferred_element_type=F32) * dq_ref[hh]
            kd = (k.astype(F32) * dk_ref[hh]).astype(BF16)
            s_ref[hh] = state * dc_ref[hh] + lax.dot_general(
                kd, v, (((0,), (0,)), ((), ())), preferred_element_type=F32)
            o_ref[rows, cols] = _head_norm_gate(inner + cross, g_ref[rows, cols], gn_ref[:, cols]).astype(o_ref.dtype)

    @pl.when(i == pl.num_programs(1) - 1)
    def _():
        sfin_ref[...] = s_ref[...]


def _ret_prompt(q, k, v, g, tabs, ret_gn, layer, tb):
    nb, t, _ = q.shape
    dmask, dq, dk, dc = tabs
    row = lambda b, i: (b, i, 0)
    c3 = lambda b, i: (0, 0, 0)
    return pl.pallas_call(
        functools.partial(_ret_prompt_kernel, n_chunks=tb // CHUNK),
        out_shape=[jax.ShapeDtypeStruct((nb, t, D_RET), BF16),
                   jax.ShapeDtypeStruct((nb, H_RET, HEAD, HEAD), F32)],
        grid=(nb, t // tb),
        in_specs=[pl.BlockSpec((None, tb, D_RET), row)] * 4 + [
            _const_spec((H_RET, CHUNK, CHUNK), c3),
            _const_spec((H_RET, CHUNK, 1), c3),
            _const_spec((H_RET, CHUNK, 1), c3),
            _const_spec((H_RET, 1, HEAD), c3),
            _const_spec((None, 1, D_RET), lambda b, i: (layer, 0, 0)),
        ],
        out_specs=[pl.BlockSpec((None, tb, D_RET), row),
                   pl.BlockSpec((None, H_RET, HEAD, HEAD), lambda b, i: (b, 0, 0, 0))],
        scratch_shapes=[pltpu.VMEM((H_RET, HEAD, HEAD), F32)],
        compiler_params=pltpu.CompilerParams(
            dimension_semantics=("arbitrary", "arbitrary"), vmem_limit_bytes=VMEM_LIMIT_BYTES),
        name="retention_prompt",
    )(q, k, v, g, dmask, dq, dk, dc, ret_gn)


def _ret_sample_kernel(*refs, bb, chained):
    q_ref, k_ref, v_ref, g_ref, gamma_ref, gn_ref, s_ref = refs[:7]
    o_ref, snew_ref, qt_ref, kt_ref = refs[-4:]
    i = pl.program_id(0)
    n = q_ref.shape[0]

    @pl.when(i == 0)
    def _():
        for hh in range(H_RET):
            cols = slice(hh * HEAD, (hh + 1) * HEAD)
            qt_ref[hh] = q_ref[:, cols].T.astype(BF16)
            kt_ref[hh] = k_ref[:, cols].T.astype(BF16)

    base = pl.multiple_of(i * bb, bb)
    seq = lax.broadcasted_iota(jnp.int32, (n, bb * HEAD), 0)
    lane_blk = lax.shift_right_logical(lax.broadcasted_iota(jnp.int32, (n, bb * HEAD), 1), HEAD.bit_length() - 1)
    sel = seq == base + lane_blk
    sel_b = jnp.where(sel, 1.0, 0.0).astype(BF16)
    rows = pl.ds(base, bb)
    for hh in range(H_RET):
        cols = slice(hh * HEAD, (hh + 1) * HEAD)
        gamma = gamma_ref[hh]
        v_tiled = jnp.concatenate([v_ref[:, cols]] * bb, axis=1)
        v_blk = jnp.where(sel, v_tiled, 0.0).astype(BF16)
        q_bc = jnp.dot(qt_ref[hh], sel_b, preferred_element_type=F32)
        kv = jnp.dot(kt_ref[hh], v_blk, preferred_element_type=F32)
        cross_rows = []
        for j in range(bb):
            blk = slice(j * HEAD, (j + 1) * HEAD)
            state = s_ref[j, hh]
            cross_rows.append(jnp.sum(q_bc[:, blk] * state, axis=0, keepdims=True))
            snew_ref[j, hh] = state * gamma + kv[:, blk]
        cross = jnp.concatenate(cross_rows, axis=0) * gamma
        qb, kb, vb = q_ref[rows, cols], k_ref[rows, cols], v_ref[rows, cols]
        inner = jnp.sum(qb * kb, axis=-1, keepdims=True) * vb
        o_ref[:, cols] = _head_norm_gate(inner + cross, g_ref[:, cols], gn_ref[:, cols])


def _ret_sample(q, k, v, g, gamma, ret_gn, state, stacked, layer, bb):
    n = q.shape[0]
    full = pl.BlockSpec((n, D_RET), lambda i: (0, 0))
    rows = pl.BlockSpec((bb, D_RET), lambda i: (i, 0))
    chained = stacked is not None
    in_specs = [
        full, full, full, rows,
        pl.BlockSpec((H_RET, 1, HEAD), lambda i: (0, 0, 0)),
        pl.BlockSpec((None, 1, D_RET), lambda i: (layer, 0, 0)),
        pl.BlockSpec((None, bb, H_RET, HEAD, HEAD), lambda i: (layer, i, 0, 0, 0)),
    ]
    args = [q, k, v, g, gamma, ret_gn, state]
    if chained:
        in_specs.append(pl.BlockSpec(memory_space=pl.ANY))
        args.append(stacked)
    return pl.pallas_call(
        functools.partial(_ret_sample_kernel, bb=bb, chained=chained),
        out_shape=[jax.ShapeDtypeStruct((n, D_RET), F32),
                   jax.ShapeDtypeStruct((DEPTH, n, H_RET, HEAD, HEAD), F32)],
        grid=(n // bb,),
        in_specs=in_specs,
        out_specs=[rows, pl.BlockSpec((None, bb, H_RET, HEAD, HEAD), lambda i: (layer, i, 0, 0, 0))],
        scratch_shapes=[pltpu.VMEM((H_RET, HEAD, n), BF16), pltpu.VMEM((H_RET, HEAD, n), BF16)],
        input_output_aliases={len(args) - 1: 1} if chained else {},
        compiler_params=pltpu.CompilerParams(
            dimension_semantics=("arbitrary",), vmem_limit_bytes=VMEM_LIMIT_BYTES),
        name="retention_sample",
    )(*args)


def _pool_project(window_sums, cur, inv_cnt, pw_ref, pscale_ref):
    ys = []
    for gi in range(len(POOL_WINDOWS)):
        cols = slice(gi * POOL_GC, (gi + 1) * POOL_GC)
        p = window_sums[gi] * inv_cnt[gi] - cur[:, cols]
        y = jnp.dot(p.astype(BF16), pw_ref[gi], preferred_element_type=F32) * pscale_ref[:, cols]
        ys.append(y.astype(BF16))
    return jnp.concatenate(ys, axis=-1)


def _mix_ffn(x, o, y, gate_m, gain_f, shift_f, scale_f, gate_f, wo_ref, wgu_ref, wd_ref, hm_ref):
    mix = jnp.dot(jnp.concatenate([o, y], axis=-1), wo_ref[...], preferred_element_type=F32)
    x1 = x + gate_m * mix
    h2 = _rmsnorm_mod(x1, gain_f, shift_f, scale_f).astype(BF16)
    for off, width in FF_CHUNKS:
        a = jnp.dot(h2, wgu_ref[:, off:off + width], preferred_element_type=F32)
        b = jnp.dot(h2, wgu_ref[:, D_FF + off:D_FF + off + width], preferred_element_type=F32)
        hm_ref[:, off:off + width] = (_silu(a) * b).astype(BF16)
    return x1 + gate_f * jnp.dot(hm_ref[...], wd_ref[...], preferred_element_type=F32)


def _mix_prompt_kernel(x_ref, o_ref, u_ref, mod_ref, gain_ref, pw_ref, pscale_ref, wo_ref, wgu_ref, wd_ref,
                       out_ref, pool_ref, ext_ref, carry_ref, hm_ref, *, tm):
    i = pl.program_id(1)

    @pl.when(i == 0)
    def _():
        carry_ref[...] = jnp.zeros_like(carry_ref)

    u = u_ref[...]
    ext_ref[0:HALO, :] = carry_ref[...]
    ext_ref[HALO:, :] = u
    carry_ref[...] = u[tm - HALO:, :]

    @pl.when(i == pl.num_programs(1) - 1)
    def _():
        pool_ref[...] = u[tm - HALO:, :]

    pos = i * tm + lax.broadcasted_iota(jnp.int32, (tm, 1), 0)
    sums, inv_cnt = [], []
    for gi, w in enumerate(POOL_WINDOWS):
        acc = ext_ref[:, gi * POOL_GC:(gi + 1) * POOL_GC]
        span = 1
        while span < w:
            acc = acc + pltpu.roll(acc, span, 0)
            span *= 2
        sums.append(acc[HALO:, :])
        inv_cnt.append(1.0 / jnp.minimum(pos + 1, w).astype(F32))
    y = _pool_project(sums, u, inv_cnt, pw_ref, pscale_ref)
    out_ref[...] = _mix_ffn(x_ref[...], o_ref[...], y, mod_ref[2], gain_ref[...], mod_ref[3], mod_ref[4],
                            mod_ref[5], wo_ref, wgu_ref, wd_ref, hm_ref)


def _mix_prompt(x, o, u, mod_p, gain_f, pool_w, pool_scale, w_out, w_gu, w_down, layer, tm):
    nb, t, _ = x.shape
    row = lambda b, i: (b, i, 0)
    lay3 = lambda b, i: (layer, 0, 0)
    return pl.pallas_call(
        functools.partial(_mix_prompt_kernel, tm=tm),
        out_shape=[jax.ShapeDtypeStruct((nb, t, D_MODEL), F32),
                   jax.ShapeDtypeStruct((nb, HALO, D_POOL), F32)],
        grid=(nb, t // tm),
        in_specs=[
            pl.BlockSpec((None, tm, D_MODEL), row),
            pl.BlockSpec((None, tm, D_RET), row),
            pl.BlockSpec((None, tm, D_POOL), row),
            pl.BlockSpec((None, None, N_MOD, 1, D_MODEL), lambda b, i: (layer, b, 0, 0, 0)),
            _const_spec((None, 1, D_MODEL), lay3),
            _const_spec((None, len(POOL_WINDOWS), POOL_GC, POOL_GC), lambda b, i: (layer, 0, 0, 0)),
            _const_spec((None, 1, D_POOL), lay3),
            _const_spec((None, D_MODEL, D_MODEL), lay3),
            _const_spec((None, D_MODEL, 2 * D_FF), lay3),
            _const_spec((None, D_FF, D_MODEL), lay3),
        ],
        out_specs=[pl.BlockSpec((None, tm, D_MODEL), row),
                   pl.BlockSpec((None, HALO, D_POOL), lambda b, i: (b, 0, 0))],
        scratch_shapes=[pltpu.VMEM((tm + HALO, D_POOL), F32),
                        pltpu.VMEM((HALO, D_POOL), F32),
                        pltpu.VMEM((tm, D_FF), BF16)],
        compiler_params=pltpu.CompilerParams(
            dimension_semantics=("arbitrary", "arbitrary"), vmem_limit_bytes=VMEM_LIMIT_BYTES),
        name="mix_ffn_prompt",
    )(x, o, u, mod_p, gain_f, pool_w, pool_scale, w_out, w_gu, w_down)


def _mix_sample_kernel(*refs):
    x_ref, o_ref, u_ref, buf_ref, mod_ref, gain_ref, pw_ref, pscale_ref, wo_ref, wgu_ref, wd_ref = refs[:11]
    out_ref, newbuf_ref, hm_ref = refs[-3:]
    u = u_ref[...]
    sums, inv_cnt = [], []
    for gi, w in enumerate(POOL_WINDOWS):
        acc = u[:, gi * POOL_GC:(gi + 1) * POOL_GC]
        for r in range(POOL_BUF - (w - 1), POOL_BUF):
            acc = acc + buf_ref[:, r * D_POOL + gi * POOL_GC:r * D_POOL + (gi + 1) * POOL_GC]
        sums.append(acc)
        inv_cnt.append(1.0 / w)
    y = _pool_project(sums, u, inv_cnt, pw_ref, pscale_ref)
    newbuf_ref[:, 0:(POOL_BUF - 1) * D_POOL] = buf_ref[:, D_POOL:]
    newbuf_ref[:, (POOL_BUF - 1) * D_POOL:] = u
    m = lambda k: mod_ref[:, k * D_MODEL:(k + 1) * D_MODEL]
    out_ref[...] = _mix_ffn(x_ref[...], o_ref[...].astype(BF16), y, m(2), gain_ref[...], m(3), m(4), m(5),
                            wo_ref, wgu_ref, wd_ref, hm_ref)


def _mix_sample(x, o, u, buf, mod_s, gain_f, pool_w, pool_scale, w_out, w_gu, w_down, stacked, layer):
    n = x.shape[0]
    full = lambda i: (0, 0)
    lay3 = lambda i: (layer, 0, 0)
    args = [x, o, u, buf, mod_s, gain_f, pool_w, pool_scale, w_out, w_gu, w_down]
    chain_specs = []
    if stacked is not None:
        chain_specs.append(pl.BlockSpec(memory_space=pl.ANY))
        args.append(stacked)
    return pl.pallas_call(
        _mix_sample_kernel,
        out_shape=[jax.ShapeDtypeStruct((n, D_MODEL), F32),
                   jax.ShapeDtypeStruct((DEPTH, n, POOL_BUF * D_POOL), F32)],
        grid=(1,),
        input_output_aliases={len(args) - 1: 1} if stacked is not None else {},
        in_specs=[
            pl.BlockSpec((n, D_MODEL), full),
            pl.BlockSpec((n, D_RET), full),
            pl.BlockSpec((n, D_POOL), full),
            _const_spec((None, n, POOL_BUF * D_POOL), lay3),
            _const_spec((None, n, N_MOD * D_MODEL), lay3),
            pl.BlockSpec((None, 1, D_MODEL), lay3),
            pl.BlockSpec((None, len(POOL_WINDOWS), POOL_GC, POOL_GC), lambda i: (layer, 0, 0, 0)),
            pl.BlockSpec((None, 1, D_POOL), lay3),
            _const_spec((None, D_MODEL, D_MODEL), lay3),
            _const_spec((None, D_MODEL, 2 * D_FF), lay3),
            _const_spec((None, D_FF, D_MODEL), lay3),
        ] + chain_specs,
        out_specs=[pl.BlockSpec((n, D_MODEL), full),
                   pl.BlockSpec((None, n, POOL_BUF * D_POOL), lay3)],
        scratch_shapes=[pltpu.VMEM((n, D_FF), BF16)],
        compiler_params=pltpu.CompilerParams(
            dimension_semantics=("arbitrary",), vmem_limit_bytes=VMEM_LIMIT_BYTES),
        name="mix_ffn_sample",
    )(*args)


def _final_norm_kernel(x_ref, gain_ref, o_ref):
    x = x_ref[...]
    ms = jnp.mean(x * x, axis=-1, keepdims=True)
    o_ref[...] = x * lax.rsqrt(ms + EPS) * gain_ref[...]


def _final_norm(x2d, gain, tm):
    n = x2d.shape[0]
    return pl.pallas_call(
        _final_norm_kernel,
        out_shape=jax.ShapeDtypeStruct(x2d.shape, F32),
        grid=(n // tm,),
        in_specs=[pl.BlockSpec((tm, D_MODEL), lambda i: (i, 0)),
                  pl.BlockSpec((1, D_MODEL), lambda i: (0, 0))],
        out_specs=pl.BlockSpec((tm, D_MODEL), lambda i: (i, 0)),
        compiler_params=pltpu.CompilerParams(
            dimension_semantics=("arbitrary",), vmem_limit_bytes=VMEM_LIMIT_BYTES),
        name="final_norm",
    )(x2d, gain)


def _rope_tables(pos):
    inv = 1.0 / (ROPE_THETA ** (jnp.arange(0, HEAD, 2, dtype=F32) / HEAD))
    ang = pos.astype(F32)[:, None] * inv[None, :]
    cos, sin = jnp.cos(ang), jnp.sin(ang)
    return jnp.concatenate([cos, cos], axis=-1), jnp.concatenate([-sin, sin], axis=-1)


def _decay_tables():
    log_gamma = jnp.log(1.0 - 2.0 ** (-5.0 - jnp.arange(H_RET, dtype=F32)))
    idx = jnp.arange(CHUNK, dtype=F32)
    diff = idx[:, None] - idx[None, :]
    dmask = jnp.where(diff[None] >= 0, jnp.exp(jnp.maximum(diff, 0.0)[None] * log_gamma[:, None, None]), 0.0)
    dq = jnp.exp((idx + 1.0)[None, :] * log_gamma[:, None])[:, :, None]
    dk = jnp.exp((CHUNK - 1.0 - idx)[None, :] * log_gamma[:, None])[:, :, None]
    dc = jnp.broadcast_to(jnp.exp(CHUNK * log_gamma)[:, None, None], (H_RET, 1, HEAD))
    gamma1 = jnp.broadcast_to(jnp.exp(log_gamma)[:, None, None], (H_RET, 1, HEAD))
    return (dmask, dq, dk, dc), gamma1


def kernel(x_prompt, x_sample, c_prompt, c_sample, state_ret, state_pool, norm_mix, norm_ffn, w_ada, b_ada,
           w_in, ret_gn, pool_w, pool_scale, w_out, w_gu, w_down, final_norm):
    nb, t, _ = x_prompt.shape
    ns = x_sample.shape[0]
    past_len = 16384
    tm = 512

    mod = _modulation(jnp.concatenate([c_prompt, c_sample], axis=0), w_ada, b_ada)
    mod_p = mod[:, :nb].reshape(DEPTH, nb, N_MOD, 1, D_MODEL)
    mod_s = mod[:, nb:]

    w_in_b, w_out_b, w_gu_b, w_down_b, pool_w_b = (w.astype(BF16) for w in (w_in, w_out, w_gu, w_down, pool_w))
    gain_m = norm_mix.reshape(DEPTH, 1, D_MODEL)
    gain_f = norm_ffn.reshape(DEPTH, 1, D_MODEL)
    gn = ret_gn.reshape(DEPTH, 1, D_RET)
    pscale = pool_scale.reshape(DEPTH, 1, D_POOL)
    cos_p, sin_p = _rope_tables(jnp.arange(t, dtype=jnp.int32))
    cos_s, sin_s = _rope_tables(jnp.full((1,), past_len, dtype=jnp.int32))
    tabs, gamma1 = _decay_tables()
    pool_state = state_pool.reshape(DEPTH, ns, POOL_BUF * D_POOL)

    xp, xs = x_prompt, x_sample.reshape(ns, D_MODEL)
    ret_p, pool_p, ret_s, pool_s = [], [], None, None
    for l in range(DEPTH):
        q, k, v, g, u = _in_proj_prompt(xp, mod_p, gain_m, w_in_b, cos_p, sin_p, l, tm)
        o, s_fin = _ret_prompt(q, k, v, g, tabs, gn, l, tm)
        xp, pool16 = _mix_prompt(xp, o, u, mod_p, gain_f, pool_w_b, pscale, w_out_b, w_gu_b, w_down_b, l, tm)
        ret_p.append(s_fin)
        pool_p.append(pool16[:, HALO - POOL_BUF:, :])

        q, k, v, g, u = _in_proj_sample(xs, mod_s, gain_m, w_in_b, cos_s, sin_s, l)
        o, ret_s = _ret_sample(q, k, v, g, gamma1, gn, state_ret, ret_s, l, 16)
        xs, pool_s = _mix_sample(xs, o, u, pool_state, mod_s, gain_f, pool_w_b, pscale, w_out_b, w_gu_b,
                                 w_down_b, pool_s, l)

    y_prompt = _final_norm(xp.reshape(nb * t, D_MODEL), final_norm.reshape(1, D_MODEL), 1024).reshape(nb, t, D_MODEL)
    y_sample = _final_norm(xs, final_norm.reshape(1, D_MODEL), ns).reshape(ns, 1, D_MODEL)
    return (y_prompt, y_sample, jnp.stack(ret_p), jnp.stack(pool_p), ret_s,
            pool_s.reshape(DEPTH, ns, POOL_BUF, D_POOL))
```

```python
import functools

import jax
import jax.numpy as jnp
from jax import lax
from jax.experimental import pallas as pl
from jax.experimental.pallas import tpu as pltpu

F32 = jnp.float32
BF16 = jnp.bfloat16

D_MODEL = 1024
DEPTH = 4
H_RET = 4
HEAD = 128
D_RET = H_RET * HEAD
D_POOL = D_MODEL - D_RET
POOL_WINDOWS = (2, 4, 8, 16)
POOL_GC = D_POOL // len(POOL_WINDOWS)
POOL_BUF = max(POOL_WINDOWS) - 1
HALO = 16
D_IN = 4 * D_RET + D_POOL
D_FF = 2816
CHUNK = 128
ROPE_THETA = 10000.0
EPS = 1e-6
N_MOD = 6
PAST_LEN = 16384

FF_CHUNKS = ((0, 512), (512, 512), (1024, 512), (1536, 512), (2048, 512), (2560, 256))

VMEM_LIMIT_BYTES = 56 * 1024 * 1024


def _silu(x):
    return x * jax.nn.sigmoid(x)


def _rmsnorm(x, gain):
    ms = jnp.mean(x * x, axis=-1, keepdims=True)
    return x * lax.rsqrt(ms + EPS) * gain


def _rmsnorm_mod(x, gain, shift, scale):
    return _rmsnorm(x, gain) * (1.0 + scale) + shift


def _const_spec(shape, index_map):
    return pl.BlockSpec(shape, index_map, pipeline_mode=pl.Buffered(1))


def _params(n_axes):
    return pltpu.CompilerParams(dimension_semantics=("arbitrary",) * n_axes, vmem_limit_bytes=VMEM_LIMIT_BYTES)


def _mod_kernel(c_ref, w_ref, b_ref, o_ref):
    sc = _silu(c_ref[...]).astype(BF16)
    o_ref[...] = jnp.dot(sc, w_ref[...].astype(BF16), preferred_element_type=F32) + b_ref[...]


def _modulation(c_all, w_ada, b_ada):
    n = c_all.shape[0]
    tn = 1024
    return pl.pallas_call(
        _mod_kernel,
        out_shape=jax.ShapeDtypeStruct((DEPTH, n, N_MOD * D_MODEL), F32),
        grid=(DEPTH, N_MOD * D_MODEL // tn),
        in_specs=[
            pl.BlockSpec((n, D_MODEL), lambda l, j: (0, 0)),
            pl.BlockSpec((None, D_MODEL, tn), lambda l, j: (l, 0, j)),
            pl.BlockSpec((None, 1, tn), lambda l, j: (l, 0, j)),
        ],
        out_specs=pl.BlockSpec((None, n, tn), lambda l, j: (l, 0, j)),
        compiler_params=_params(2),
        name="adaln_modulation",
    )(c_all, w_ada, b_ada.reshape(DEPTH, 1, N_MOD * D_MODEL))


def _rope_store(z, cos, sin, out_ref, post_scale):
    for hh in range(H_RET):
        cols = slice(hh * HEAD, (hh + 1) * HEAD)
        zh = z[:, cols]
        r = zh * cos + pltpu.roll(zh, HEAD // 2, 1) * sin
        if post_scale is not None:
            r = r * post_scale
        out_ref[:, cols] = r.astype(out_ref.dtype)


def _in_proj(x, gain, shift, scale, w_ref, cos, sin, q_ref, k_ref, v_ref, g_ref, u_store):
    h = _rmsnorm_mod(x, gain, shift, scale).astype(BF16)

    def proj(sec):
        return jnp.dot(h, w_ref[:, sec * D_RET:(sec + 1) * D_RET], preferred_element_type=F32)

    _rope_store(proj(0), cos, sin, q_ref, None)
    _rope_store(proj(1), cos, sin, k_ref, HEAD ** -0.5)
    v_ref[...] = proj(2).astype(v_ref.dtype)
    g_ref[...] = proj(3)
    u_store(proj(4))


def _head_norm_gate(o, g, gain):
    return _rmsnorm(o, gain) * _silu(g)


def _pool_project(window_sums, cur, inv_cnt, pw_ref, pscale_ref):
    ys = []
    for gi in range(len(POOL_WINDOWS)):
        cols = slice(gi * POOL_GC, (gi + 1) * POOL_GC)
        p = window_sums[gi] * inv_cnt[gi] - cur[:, cols]
        y = jnp.dot(p.astype(BF16), pw_ref[gi], preferred_element_type=F32) * pscale_ref[:, cols]
        ys.append(y.astype(BF16))
    return jnp.concatenate(ys, axis=-1)


def _mix_ffn(x, o, y, gate_m, gain_f, shift_f, scale_f, gate_f, wo_ref, wgu_ref, wd_ref, hm_ref):
    mix = jnp.dot(jnp.concatenate([o, y], axis=-1), wo_ref[...], preferred_element_type=F32)
    x1 = x + gate_m * mix
    h2 = _rmsnorm_mod(x1, gain_f, shift_f, scale_f).astype(BF16)
    for off, width in FF_CHUNKS:
        a = jnp.dot(h2, wgu_ref[:, off:off + width], preferred_element_type=F32)
        b = jnp.dot(h2, wgu_ref[:, D_FF + off:D_FF + off + width], preferred_element_type=F32)
        hm_ref[:, off:off + width] = (_silu(a) * b).astype(BF16)
    return x1 + gate_f * jnp.dot(hm_ref[...], wd_ref[...], preferred_element_type=F32)


def _layer_prompt_kernel(x_ref, mod_ref, gain_m_ref, gain_f_ref, w_in_ref, cos_ref, sin_ref,
                         dmask_ref, dq_ref, dk_ref, dc_ref, gn_ref, pw_ref, pscale_ref,
                         wo_ref, wgu_ref, wd_ref, fin_ref,
                         out_ref, sfin_ref, pool_ref,
                         q_ref, k_ref, v_ref, g_ref, o_ref, ext_ref, carry_ref, hm_ref, s_ref,
                         *, tm, apply_final_norm):
    i = pl.program_id(1)
    last = pl.num_programs(1) - 1

    @pl.when(i == 0)
    def _():
        s_ref[...] = jnp.zeros_like(s_ref)
        carry_ref[...] = jnp.zeros_like(carry_ref)

    ext_ref[0:HALO, :] = carry_ref[...]

    def u_store(u):
        ext_ref[HALO:, :] = u

    _in_proj(x_ref[...], gain_m_ref[...], mod_ref[0], mod_ref[1], w_in_ref, cos_ref[...], sin_ref[...],
             q_ref, k_ref, v_ref, g_ref, u_store)
    carry_ref[...] = ext_ref[tm:, :]

    @pl.when(i == last)
    def _():
        pool_ref[...] = ext_ref[tm:, :]

    for c in range(tm // CHUNK):
        rows = slice(c * CHUNK, (c + 1) * CHUNK)
        for hh in range(H_RET):
            cols = slice(hh * HEAD, (hh + 1) * HEAD)
            q, k, v = q_ref[rows, cols], k_ref[rows, cols], v_ref[rows, cols]
            state = s_ref[hh]
            s = lax.dot_general(q, k, (((1,), (1,)), ((), ())), preferred_element_type=F32) * dmask_ref[hh]
            inner = jnp.dot(s.astype(BF16), v, preferred_element_type=F32)
            cross = jnp.dot(q, state.astype(BF16), preferred_element_type=F32) * dq_ref[hh]
            kd = (k.astype(F32) * dk_ref[hh]).astype(BF16)
            s_ref[hh] = state * dc_ref[hh] + lax.dot_general(
                kd, v, (((0,), (0,)), ((), ())), preferred_element_type=F32)
            o_ref[rows, cols] = _head_norm_gate(inner + cross, g_ref[rows, cols], gn_ref[:, cols]).astype(BF16)

    @pl.when(i == last)
    def _():
        sfin_ref[...] = s_ref[...]

    pos = i * tm + lax.broadcasted_iota(jnp.int32, (tm, 1), 0)
    sums, inv_cnt = [], []
    for gi, w in enumerate(POOL_WINDOWS):
        acc = ext_ref[:, gi * POOL_GC:(gi + 1) * POOL_GC]
        span = 1
        while span < w:
            acc = acc + pltpu.roll(acc, span, 0)
            span *= 2
        sums.append(acc[HALO:, :])
        inv_cnt.append(1.0 / jnp.minimum(pos + 1, w).astype(F32))
    y = _pool_project(sums, ext_ref[HALO:, :], inv_cnt, pw_ref, pscale_ref)

    x2 = _mix_ffn(x_ref[...], o_ref[...], y, mod_ref[2], gain_f_ref[...], mod_ref[3], mod_ref[4], mod_ref[5],
                  wo_ref, wgu_ref, wd_ref, hm_ref)
    out_ref[...] = _rmsnorm(x2, fin_ref[...]) if apply_final_norm else x2


def _layer_prompt(x, mod_p, gain_m, gain_f, w_in, cos, sin, tabs, ret_gn, pool_w, pool_scale, w_out, w_gu,
                  w_down, fin_gain, layer, tm):
    nb, t, _ = x.shape
    dmask, dq, dk, dc = tabs
    row = lambda b, i: (b, i, 0)
    tab = lambda b, i: (i, 0)
    lay3 = lambda b, i: (layer, 0, 0)
    c3 = lambda b, i: (0, 0, 0)
    return pl.pallas_call(
        functools.partial(_layer_prompt_kernel, tm=tm, apply_final_norm=layer == DEPTH - 1),
        out_shape=[jax.ShapeDtypeStruct((nb, t, D_MODEL), F32),
                   jax.ShapeDtypeStruct((nb, H_RET, HEAD, HEAD), F32),
                   jax.ShapeDtypeStruct((nb, HALO, D_POOL), F32)],
        grid=(nb, t // tm),
        in_specs=[
            pl.BlockSpec((None, tm, D_MODEL), row),
            pl.BlockSpec((None, None, N_MOD, 1, D_MODEL), lambda b, i: (layer, b, 0, 0, 0)),
            _const_spec((None, 1, D_MODEL), lay3),
            _const_spec((None, 1, D_MODEL), lay3),
            _const_spec((None, D_MODEL, D_IN), lay3),
            pl.BlockSpec((tm, HEAD), tab),
            pl.BlockSpec((tm, HEAD), tab),
            _const_spec((H_RET, CHUNK, CHUNK), c3),
            _const_spec((H_RET, CHUNK, 1), c3),
            _const_spec((H_RET, CHUNK, 1), c3),
            _const_spec((H_RET, 1, HEAD), c3),
            _const_spec((None, 1, D_RET), lay3),
            _const_spec((None, len(POOL_WINDOWS), POOL_GC, POOL_GC), lambda b, i: (layer, 0, 0, 0)),
            _const_spec((None, 1, D_POOL), lay3),
            _const_spec((None, D_MODEL, D_MODEL), lay3),
            _const_spec((None, D_MODEL, 2 * D_FF), lay3),
            _const_spec((None, D_FF, D_MODEL), lay3),
            _const_spec((1, D_MODEL), lambda b, i: (0, 0)),
        ],
        out_specs=[pl.BlockSpec((None, tm, D_MODEL), row),
                   pl.BlockSpec((None, H_RET, HEAD, HEAD), lambda b, i: (b, 0, 0, 0)),
                   pl.BlockSpec((None, HALO, D_POOL), lambda b, i: (b, 0, 0))],
        scratch_shapes=[pltpu.VMEM((tm, D_RET), BF16),
                        pltpu.VMEM((tm, D_RET), BF16),
                        pltpu.VMEM((tm, D_RET), BF16),
                        pltpu.VMEM((tm, D_RET), F32),
                        pltpu.VMEM((tm, D_RET), BF16),
                        pltpu.VMEM((tm + HALO, D_POOL), F32),
                        pltpu.VMEM((HALO, D_POOL), F32),
                        pltpu.VMEM((tm, D_FF), BF16),
                        pltpu.VMEM((H_RET, HEAD, HEAD), F32)],
        compiler_params=_params(2),
        name="layer_prompt",
    )(x, mod_p, gain_m, gain_f, w_in, cos, sin, dmask, dq, dk, dc, ret_gn, pool_w, pool_scale, w_out, w_gu,
      w_down, fin_gain)


def _in_proj_sample_kernel(x_ref, mod_ref, gain_ref, w_ref, cos_ref, sin_ref, q_ref, k_ref, v_ref, g_ref, u_ref):
    def u_store(u):
        u_ref[...] = u

    _in_proj(x_ref[...], gain_ref[...], mod_ref[:, 0:D_MODEL], mod_ref[:, D_MODEL:2 * D_MODEL], w_ref,
             cos_ref[...], sin_ref[...], q_ref, k_ref, v_ref, g_ref, u_store)


def _in_proj_sample(x, mod_s, gain, w_in, cos, sin, layer):
    n = x.shape[0]
    outs = [jax.ShapeDtypeStruct((n, D_RET), F32)] * 5
    return pl.pallas_call(
        _in_proj_sample_kernel,
        out_shape=outs,
        grid=(1,),
        in_specs=[
            pl.BlockSpec((n, D_MODEL), lambda i: (0, 0)),
            pl.BlockSpec((None, n, N_MOD * D_MODEL), lambda i: (layer, 0, 0)),
            pl.BlockSpec((None, 1, D_MODEL), lambda i: (layer, 0, 0)),
            _const_spec((None, D_MODEL, D_IN), lambda i: (layer, 0, 0)),
            pl.BlockSpec((1, HEAD), lambda i: (0, 0)),
            pl.BlockSpec((1, HEAD), lambda i: (0, 0)),
        ],
        out_specs=[pl.BlockSpec((n, D_RET), lambda i: (0, 0))] * 5,
        compiler_params=_params(1),
        name="in_proj_sample",
    )(x, mod_s, gain, w_in, cos, sin)


def _ret_sample_kernel(*refs, bb):
    q_ref, k_ref, v_ref, g_ref, gamma_ref, gn_ref, s_ref = refs[:7]
    o_ref, snew_ref, qt_ref, kt_ref = refs[-4:]
    i = pl.program_id(0)
    n = q_ref.shape[0]

    @pl.when(i == 0)
    def _():
        for hh in range(H_RET):
            cols = slice(hh * HEAD, (hh + 1) * HEAD)
            qt_ref[hh] = q_ref[:, cols].T.astype(BF16)
            kt_ref[hh] = k_ref[:, cols].T.astype(BF16)

    base = pl.multiple_of(i * bb, bb)
    seq = lax.broadcasted_iota(jnp.int32, (n, bb * HEAD), 0)
    lane_blk = lax.shift_right_logical(lax.broadcasted_iota(jnp.int32, (n, bb * HEAD), 1), HEAD.bit_length() - 1)
    sel = seq == base + lane_blk
    sel_b = jnp.where(sel, 1.0, 0.0).astype(BF16)
    rows = pl.ds(base, bb)
    for hh in range(H_RET):
        cols = slice(hh * HEAD, (hh + 1) * HEAD)
        gamma = gamma_ref[hh]
        v_tiled = jnp.concatenate([v_ref[:, cols]] * bb, axis=1)
        v_blk = jnp.where(sel, v_tiled, 0.0).astype(BF16)
        q_bc = jnp.dot(qt_ref[hh], sel_b, preferred_element_type=F32)
        kv = jnp.dot(kt_ref[hh], v_blk, preferred_element_type=F32)
        cross_rows = []
        for j in range(bb):
            blk = slice(j * HEAD, (j + 1) * HEAD)
            state = s_ref[j, hh]
            cross_rows.append(jnp.sum(q_bc[:, blk] * state, axis=0, keepdims=True))
            snew_ref[j, hh] = state * gamma + kv[:, blk]
        cross = jnp.concatenate(cross_rows, axis=0) * gamma
        qb, kb, vb = q_ref[rows, cols], k_ref[rows, cols], v_ref[rows, cols]
        inner = jnp.sum(qb * kb, axis=-1, keepdims=True) * vb
        o_ref[:, cols] = _head_norm_gate(inner + cross, g_ref[:, cols], gn_ref[:, cols])


def _ret_sample(q, k, v, g, gamma, ret_gn, state, stacked, layer, bb):
    n = q.shape[0]
    full = pl.BlockSpec((n, D_RET), lambda i: (0, 0))
    rows = pl.BlockSpec((bb, D_RET), lambda i: (i, 0))
    in_specs = [
        full, full, full, rows,
        pl.BlockSpec((H_RET, 1, HEAD), lambda i: (0, 0, 0)),
        pl.BlockSpec((None, 1, D_RET), lambda i: (layer, 0, 0)),
        pl.BlockSpec((None, bb, H_RET, HEAD, HEAD), lambda i: (layer, i, 0, 0, 0)),
    ]
    args = [q, k, v, g, gamma, ret_gn, state]
    if stacked is not None:
        in_specs.append(pl.BlockSpec(memory_space=pl.ANY))
        args.append(stacked)
    return pl.pallas_call(
        functools.partial(_ret_sample_kernel, bb=bb),
        out_shape=[jax.ShapeDtypeStruct((n, D_RET), F32),
                   jax.ShapeDtypeStruct((DEPTH, n, H_RET, HEAD, HEAD), F32)],
        grid=(n // bb,),
        in_specs=in_specs,
        out_specs=[rows, pl.BlockSpec((None, bb, H_RET, HEAD, HEAD), lambda i: (layer, i, 0, 0, 0))],
        scratch_shapes=[pltpu.VMEM((H_RET, HEAD, n), BF16), pltpu.VMEM((H_RET, HEAD, n), BF16)],
        input_output_aliases={len(args) - 1: 1} if stacked is not None else {},
        compiler_params=_params(1),
        name="retention_sample",
    )(*args)


def _mix_sample_kernel(*refs, apply_final_norm):
    (x_ref, o_ref, u_ref, buf_ref, mod_ref, gain_ref, pw_ref, pscale_ref, wo_ref, wgu_ref, wd_ref,
     fin_ref) = refs[:12]
    out_ref, newbuf_ref, hm_ref = refs[-3:]
    u = u_ref[...]
    sums, inv_cnt = [], []
    for gi, w in enumerate(POOL_WINDOWS):
        acc = u[:, gi * POOL_GC:(gi + 1) * POOL_GC]
        for r in range(POOL_BUF - (w - 1), POOL_BUF):
            acc = acc + buf_ref[:, r * D_POOL + gi * POOL_GC:r * D_POOL + (gi + 1) * POOL_GC]
        sums.append(acc)
        inv_cnt.append(1.0 / w)
    y = _pool_project(sums, u, inv_cnt, pw_ref, pscale_ref)
    newbuf_ref[:, 0:(POOL_BUF - 1) * D_POOL] = buf_ref[:, D_POOL:]
    newbuf_ref[:, (POOL_BUF - 1) * D_POOL:] = u
    m = lambda k: mod_ref[:, k * D_MODEL:(k + 1) * D_MODEL]
    x2 = _mix_ffn(x_ref[...], o_ref[...].astype(BF16), y, m(2), gain_ref[...], m(3), m(4), m(5),
                  wo_ref, wgu_ref, wd_ref, hm_ref)
    out_ref[...] = _rmsnorm(x2, fin_ref[...]) if apply_final_norm else x2


def _mix_sample(x, o, u, buf, mod_s, gain_f, pool_w, pool_scale, w_out, w_gu, w_down, fin_gain, stacked, layer):
    n = x.shape[0]
    full = lambda i: (0, 0)
    lay3 = lambda i: (layer, 0, 0)
    args = [x, o, u, buf, mod_s, gain_f, pool_w, pool_scale, w_out, w_gu, w_down, fin_gain]
    chain_specs = []
    if stacked is not None:
        chain_specs.append(pl.BlockSpec(memory_space=pl.ANY))
        args.append(stacked)
    return pl.pallas_call(
        functools.partial(_mix_sample_kernel, apply_final_norm=layer == DEPTH - 1),
        out_shape=[jax.ShapeDtypeStruct((n, D_MODEL), F32),
                   jax.ShapeDtypeStruct((DEPTH, n, POOL_BUF * D_POOL), F32)],
        grid=(1,),
        input_output_aliases={len(args) - 1: 1} if stacked is not None else {},
        in_specs=[
            pl.BlockSpec((n, D_MODEL), full),
            pl.BlockSpec((n, D_RET), full),
            pl.BlockSpec((n, D_POOL), full),
            _const_spec((None, n, POOL_BUF * D_POOL), lay3),
            _const_spec((None, n, N_MOD * D_MODEL), lay3),
            pl.BlockSpec((None, 1, D_MODEL), lay3),
            pl.BlockSpec((None, len(POOL_WINDOWS), POOL_GC, POOL_GC), lambda i: (layer, 0, 0, 0)),
            pl.BlockSpec((None, 1, D_POOL), lay3),
            _const_spec((None, D_MODEL, D_MODEL), lay3),
            _const_spec((None, D_MODEL, 2 * D_FF), lay3),
            _const_spec((None, D_FF, D_MODEL), lay3),
            pl.BlockSpec((1, D_MODEL), full),
        ] + chain_specs,
        out_specs=[pl.BlockSpec((n, D_MODEL), full),
                   pl.BlockSpec((None, n, POOL_BUF * D_POOL), lay3)],
        scratch_shapes=[pltpu.VMEM((n, D_FF), BF16)],
        compiler_params=_params(1),
        name="mix_ffn_sample",
    )(*args)


def _rope_tables(pos):
    inv = 1.0 / (ROPE_THETA ** (jnp.arange(0, HEAD, 2, dtype=F32) / HEAD))
    ang = pos.astype(F32)[:, None] * inv[None, :]
    cos, sin = jnp.cos(ang), jnp.sin(ang)
    return jnp.concatenate([cos, cos], axis=-1), jnp.concatenate([-sin, sin], axis=-1)


def _decay_tables():
    log_gamma = jnp.log(1.0 - 2.0 ** (-5.0 - jnp.arange(H_RET, dtype=F32)))
    idx = jnp.arange(CHUNK, dtype=F32)
    diff = idx[:, None] - idx[None, :]
    dmask = jnp.where(diff[None] >= 0, jnp.exp(jnp.maximum(diff, 0.0)[None] * log_gamma[:, None, None]), 0.0)
    dq = jnp.exp((idx + 1.0)[None, :] * log_gamma[:, None])[:, :, None]
    dk = jnp.exp((CHUNK - 1.0 - idx)[None, :] * log_gamma[:, None])[:, :, None]
    dc = jnp.broadcast_to(jnp.exp(CHUNK * log_gamma)[:, None, None], (H_RET, 1, HEAD))
    gamma1 = jnp.broadcast_to(jnp.exp(log_gamma)[:, None, None], (H_RET, 1, HEAD))
    return (dmask, dq, dk, dc), gamma1


def kernel(x_prompt, x_sample, c_prompt, c_sample, state_ret, state_pool, norm_mix, norm_ffn, w_ada, b_ada,
           w_in, ret_gn, pool_w, pool_scale, w_out, w_gu, w_down, final_norm):
    nb, t, _ = x_prompt.shape
    ns = x_sample.shape[0]
    tm = 512

    mod = _modulation(jnp.concatenate([c_prompt, c_sample], axis=0), w_ada, b_ada)
    mod_p = mod[:, :nb].reshape(DEPTH, nb, N_MOD, 1, D_MODEL)
    mod_s = mod[:, nb:]

    w_in_b, w_out_b, w_gu_b, w_down_b, pool_w_b = (w.astype(BF16) for w in (w_in, w_out, w_gu, w_down, pool_w))
    gain_m = norm_mix.reshape(DEPTH, 1, D_MODEL)
    gain_f = norm_ffn.reshape(DEPTH, 1, D_MODEL)
    fin_gain = final_norm.reshape(1, D_MODEL)
    gn = ret_gn.reshape(DEPTH, 1, D_RET)
    pscale = pool_scale.reshape(DEPTH, 1, D_POOL)
    cos_p, sin_p = _rope_tables(jnp.arange(t, dtype=jnp.int32))
    cos_s, sin_s = _rope_tables(jnp.full((1,), PAST_LEN, dtype=jnp.int32))
    tabs, gamma1 = _decay_tables()
    pool_state = state_pool.reshape(DEPTH, ns, POOL_BUF * D_POOL)

    xp, xs = x_prompt, x_sample.reshape(ns, D_MODEL)
    ret_p, pool_p, ret_s, pool_s = [], [], None, None
    for l in range(DEPTH):
        xp, s_fin, pool16 = _layer_prompt(xp, mod_p, gain_m, gain_f, w_in_b, cos_p, sin_p, tabs, gn, pool_w_b,
                                          pscale, w_out_b, w_gu_b, w_down_b, fin_gain, l, tm)
        ret_p.append(s_fin)
        pool_p.append(pool16[:, HALO - POOL_BUF:, :])

        q, k, v, g, u = _in_proj_sample(xs, mod_s, gain_m, w_in_b, cos_s, sin_s, l)
        o, ret_s = _ret_sample(q, k, v, g, gamma1, gn, state_ret, ret_s, l, 16)
        xs, pool_s = _mix_sample(xs, o, u, pool_state, mod_s, gain_f, pool_w_b, pscale, w_out_b, w_gu_b,
                                 w_down_b, fin_gain, pool_s, l)

    return (xp, xs.reshape(ns, 1, D_MODEL), jnp.stack(ret_p), jnp.stack(pool_p), ret_s,
            pool_s.reshape(DEPTH, ns, POOL_BUF, D_POOL))
```
